```python
import jax, jax.numpy as jnp
from jax import lax
import numpy as np

D_MODEL = 1024
BATCH = 8
SEQ = 4096
DEPTH = 2

N_MIXERS = 2
N_A_LAYERS = (DEPTH + 1) // 2
N_B_LAYERS = DEPTH // 2
CONV_W = 3
A_GROUPS = 16
CHUNK = 128
D_INNER = D_MODEL
B_HEADS = 8
B_HEAD_DIM = D_INNER // B_HEADS
N_GROUPS = 4
EXPERTS_PER_GROUP = 8
N_EXPERTS = N_GROUPS * EXPERTS_PER_GROUP
TOP_K = 2
D_EXPERT = 512
ROUTE_BLOCK = 128
EPS = 1e-6

kernel_name = "hybrid_shortconv_chunkgmlp_hiermoe"


def rmsnorm(x, g):
    xf = x.astype(jnp.float32)
    y = xf * lax.rsqrt(jnp.mean(xf * xf, axis=-1, keepdims=True) + EPS)
    return (y * g.astype(jnp.float32)).astype(x.dtype)


def layernorm(x, g, b):
    xf = x.astype(jnp.float32)
    mu = jnp.mean(xf, axis=-1, keepdims=True)
    var = jnp.mean(jnp.square(xf - mu), axis=-1, keepdims=True)
    y = (xf - mu) * lax.rsqrt(var + EPS)
    return (y * g.astype(jnp.float32) + b.astype(jnp.float32)).astype(x.dtype)


def short_conv_mixer(x, w_in, conv_w, w_out):
    d = x.shape[-1]
    proj = x @ w_in
    b_gate, c_gate, h = proj[..., :d], proj[..., d:2 * d], proj[..., 2 * d:]
    z = c_gate * h
    zp = jnp.pad(z, ((0, 0), (CONV_W - 1, 0), (0, 0)))
    s = z.shape[1]
    conv = sum(conv_w[k] * zp[:, k:k + s] for k in range(CONV_W))
    return (b_gate * conv) @ w_out


def chunk_sgu_mixer(x, w_in, b_in, v_g, v_b, w_s, b_s, w_out):
    bsz, s, _ = x.shape
    z = jax.nn.gelu(x @ w_in + b_in)
    u, v = z[..., :D_INNER], z[..., D_INNER:]
    v = layernorm(v, v_g, v_b)
    n_chunks = s // CHUNK
    v = v.reshape(bsz, n_chunks, CHUNK, B_HEADS, B_HEAD_DIM)
    causal = jnp.tril(jnp.ones((CHUNK, CHUNK), dtype=w_s.dtype))
    mixed = jnp.einsum('hts,bcshd->bcthd', w_s * causal, v)
    mixed = mixed + b_s.T[None, None, :, :, None]
    mixed = mixed.reshape(bsz, s, D_INNER)
    return (u * mixed) @ w_out


def hier_moe(x, w_group, b_group, w_expert, b_expert, w_gate, w_up, w_down):
    bsz, s, d = x.shape
    xt = x.reshape(-1, d)
    t = xt.shape[0]
    g_probs = jax.nn.softmax((xt @ w_group + b_group).astype(jnp.float32), axis=-1)
    g_idx = jnp.argmax(g_probs, axis=-1)
    g_p = jnp.take_along_axis(g_probs, g_idx[:, None], axis=1)[:, 0]
    e_logits = (xt @ w_expert + b_expert).astype(jnp.float32).reshape(t, N_GROUPS, EXPERTS_PER_GROUP)
    e_logits = jnp.take_along_axis(e_logits, g_idx[:, None, None], axis=1)[:, 0]
    e_probs = jax.nn.softmax(e_logits, axis=-1)
    top_p, top_i = lax.top_k(e_probs, TOP_K)
    gates = g_p[:, None] * top_p / jnp.sum(top_p, axis=-1, keepdims=True)
    expert_id = (g_idx[:, None] * EXPERTS_PER_GROUP + top_i).reshape(-1)
    flat_tok = jnp.repeat(jnp.arange(t, dtype=jnp.int32), TOP_K)
    flat_gate = gates.reshape(-1)
    n_assign = t * TOP_K
    order = jnp.argsort(expert_id, stable=True)
    se = expert_id[order]
    counts = jnp.bincount(expert_id, length=N_EXPERTS)
    padded = ((counts + ROUTE_BLOCK - 1) // ROUTE_BLOCK) * ROUTE_BLOCK
    pad_end = jnp.cumsum(padded)
    pad_start = pad_end - padded
    start = jnp.cumsum(counts) - counts
    slot = pad_start[se] + jnp.arange(n_assign) - start[se]
    n_slots = n_assign + N_EXPERTS * ROUTE_BLOCK
    n_blocks = n_slots // ROUTE_BLOCK
    slot_tok = jnp.zeros((n_slots,), jnp.int32).at[slot].set(flat_tok[order])
    slot_gate = jnp.zeros((n_slots,), x.dtype).at[slot].set(flat_gate[order].astype(x.dtype))
    block_e = jnp.minimum(
        jnp.searchsorted(pad_end, jnp.arange(n_blocks) * ROUTE_BLOCK, side='right'), N_EXPERTS - 1)
    xs = xt[slot_tok].reshape(n_blocks, ROUTE_BLOCK, d)

    def expert_block(args):
        xb, e = args
        h = jax.nn.silu(xb @ w_gate[e]) * (xb @ w_up[e])
        return h @ w_down[e]

    ys = lax.map(expert_block, (xs, block_e)).reshape(n_slots, d)
    y = jnp.zeros((t, d), x.dtype).at[slot_tok].add(ys * slot_gate[:, None])
    return y.reshape(bsz, s, d)


def setup_inputs(seed: int = 0) -> dict:
    key = jax.random.key(seed)
    ks = jax.random.split(key, 24)
    f32 = jnp.float32
    D, F, E, G = D_MODEL, D_EXPERT, N_EXPERTS, N_GROUPS

    def nrm(k, shape, scale):
        return jax.random.normal(k, shape, f32) * scale

    return {
        "x": jax.random.normal(ks[0], (BATCH, SEQ, D), f32),
        "ln_a": 1.0 + nrm(ks[1], (N_A_LAYERS, D), 0.02),
        "w_in_a": nrm(ks[2], (N_A_LAYERS, D, 3 * D), D ** -0.5),
        "conv_w_a": nrm(ks[3], (N_A_LAYERS, CONV_W, D), CONV_W ** -0.5),
        "w_out_a": nrm(ks[4], (N_A_LAYERS, D, D), D ** -0.5),
        "ln_b": 1.0 + nrm(ks[5], (N_B_LAYERS, D), 0.02),
        "w_in_b": nrm(ks[6], (N_B_LAYERS, D, 2 * D_INNER), D ** -0.5),
        "b_in_b": nrm(ks[7], (N_B_LAYERS, 2 * D_INNER), 0.02),
        "v_ln_g": 1.0 + nrm(ks[8], (N_B_LAYERS, D_INNER), 0.02),
        "v_ln_b": nrm(ks[9], (N_B_LAYERS, D_INNER), 0.02),
        "w_s": nrm(ks[10], (N_B_LAYERS, B_HEADS, CHUNK, CHUNK), CHUNK ** -0.5),
        "b_s": 1.0 + nrm(ks[11], (N_B_LAYERS, B_HEADS, CHUNK), 0.02),
        "w_out_b": nrm(ks[12], (N_B_LAYERS, D_INNER, D), D_INNER ** -0.5),
        "ln2": 1.0 + nrm(ks[13], (DEPTH, D), 0.02),
        "w_group": nrm(ks[14], (DEPTH, D, G), D ** -0.5),
        "b_group": nrm(ks[15], (DEPTH, G), 0.01),
        "w_expert": nrm(ks[16], (DEPTH, D, E), D ** -0.5),
        "b_expert": nrm(ks[17], (DEPTH, E), 0.01),
        "w_gate": nrm(ks[18], (DEPTH, E, D, F), D ** -0.5),
        "w_up": nrm(ks[19], (DEPTH, E, D, F), D ** -0.5),
        "w_down": nrm(ks[20], (DEPTH, E, F, D), F ** -0.5),
        "ln_f": 1.0 + nrm(ks[21], (D,), 0.02),
    }


def reference(x, ln_a, w_in_a, conv_w_a, w_out_a, ln_b, w_in_b, b_in_b, v_ln_g, v_ln_b,
              w_s, b_s, w_out_b, ln2, w_group, b_group, w_expert, b_expert,
              w_gate, w_up, w_down, ln_f):
    for i in range(DEPTH):
        j = i // N_MIXERS
        if i % N_MIXERS == 0:
            x = x + short_conv_mixer(rmsnorm(x, ln_a[j]), w_in_a[j], conv_w_a[j], w_out_a[j])
        else:
            x = x + chunk_sgu_mixer(rmsnorm(x, ln_b[j]), w_in_b[j], b_in_b[j], v_ln_g[j],
                                    v_ln_b[j], w_s[j], b_s[j], w_out_b[j])
        x = x + hier_moe(rmsnorm(x, ln2[i]), w_group[i], b_group[i], w_expert[i], b_expert[i],
                         w_gate[i], w_up[i], w_down[i])
    return rmsnorm(x, ln_f)
```

```python
import functools

import numpy as np
import jax
import jax.numpy as jnp
from jax import lax
from jax.experimental import pallas as pl
from jax.experimental.pallas import tpu as pltpu

D_MODEL = 1024
CONV_W = 3
CHUNK = 128
B_HEADS = 8
N_GROUPS = 4
EXPERTS_PER_GROUP = 8
N_EXPERTS = N_GROUPS * EXPERTS_PER_GROUP
D_EXPERT = 512
EPS = 1e-6

PAIRS_PER_GROUP = EXPERTS_PER_GROUP * (EXPERTS_PER_GROUP - 1) // 2
N_BUCKETS = N_GROUPS * PAIRS_PER_GROUP
BUCKET_ROWS = 128
LOGIT_ROWS = 128
EXPERT_ROW0 = 8

SEQ_TILE = 512
BLOCK_ROWS = 128
VMEM_LIMIT = 56 * 1024 * 1024

_PAIR_A = np.array([a for a in range(EXPERTS_PER_GROUP) for b in range(a + 1, EXPERTS_PER_GROUP)], np.int32)
_PAIR_B = np.array([b for a in range(EXPERTS_PER_GROUP) for b in range(a + 1, EXPERTS_PER_GROUP)], np.int32)

f32 = jnp.float32
bf16 = jnp.bfloat16


def _dot(a, b):
    return jnp.dot(a, b, preferred_element_type=f32)


def _rms(x, g):
    return x * lax.rsqrt(jnp.mean(x * x, axis=-1, keepdims=True) + EPS) * g


def _router_logits(x1, ln2, wr):
    xn = _rms(x1, ln2).astype(bf16)
    return lax.dot_general(wr, xn, (((1,), (1,)), ((), ())), preferred_element_type=f32)


def _mixer_a_kernel(x_ref, ln_ref, win_ref, cw_ref, wout_ref, ln2_ref, wr_ref,
                    x1_ref, lg_ref, zprev_ref):
    ts = x_ref.shape[0]
    d = D_MODEL

    @pl.when(pl.program_id(1) == 0)
    def _():
        zprev_ref[...] = jnp.zeros_like(zprev_ref)

    x = x_ref[...]
    xb = _rms(x, ln_ref[...]).astype(bf16)
    b_gate = _dot(xb, win_ref[:, 0:d])
    c_gate = _dot(xb, win_ref[:, d:2 * d])
    h = _dot(xb, win_ref[:, 2 * d:3 * d])
    z = c_gate * h
    prev = zprev_ref[...]
    row = lax.broadcasted_iota(jnp.int32, (ts, 1), 0)
    z1 = jnp.where(row == 0, prev[7:8], pltpu.roll(z, 1, 0))
    z2 = jnp.where(row == 0, prev[6:7], jnp.where(row == 1, prev[7:8], pltpu.roll(z, 2, 0)))
    cw = cw_ref[...]
    conv = cw[0:1] * z2 + cw[1:2] * z1 + cw[2:3] * z
    zprev_ref[...] = z[ts - 8:ts]
    y = _dot((b_gate * conv).astype(bf16), wout_ref[...])
    x1 = x + y
    x1_ref[...] = x1
    lg_ref[...] = _router_logits(x1, ln2_ref[...], wr_ref[...])


def _mixer_a(x, ln, win, cw, wout, ln2, wr):
    bsz, s, d = x.shape
    ts = SEQ_TILE
    t = bsz * s
    nj = s // ts
    const = lambda *_: (0, 0)
    return pl.pallas_call(
        _mixer_a_kernel,
        out_shape=(jax.ShapeDtypeStruct((bsz, s, d), f32),
                   jax.ShapeDtypeStruct((LOGIT_ROWS, t), f32)),
        grid=(bsz, nj),
        in_specs=[
            pl.BlockSpec((None, ts, d), lambda b, j: (b, j, 0)),
            pl.BlockSpec((1, d), const),
            pl.BlockSpec((d, 3 * d), const),
            pl.BlockSpec((CONV_W, d), const),
            pl.BlockSpec((d, d), const),
            pl.BlockSpec((1, d), const),
            pl.BlockSpec((LOGIT_ROWS, d), const),
        ],
        out_specs=(pl.BlockSpec((None, ts, d), lambda b, j: (b, j, 0)),
                   pl.BlockSpec((LOGIT_ROWS, ts), lambda b, j: (0, b * nj + j))),
        scratch_shapes=[pltpu.VMEM((8, d), f32)],
        compiler_params=pltpu.CompilerParams(
            dimension_semantics=("arbitrary", "arbitrary"), vmem_limit_bytes=VMEM_LIMIT),
        name="mixer_a",
    )(x, ln, win, cw, wout, ln2, wr)


def _mixer_b_kernel(x_ref, ln_ref, win_ref, bin_ref, vg_ref, vb_ref, ws_ref, bs_ref,
                    wout_ref, ln2_ref, wr_ref, x1_ref, lg_ref, mixed_ref):
    ts = x_ref.shape[0]
    d = D_MODEL
    x = x_ref[...]
    xb = _rms(x, ln_ref[...]).astype(bf16)
    u = jax.nn.gelu(_dot(xb, win_ref[:, 0:d]) + bin_ref[:, 0:d])
    v = jax.nn.gelu(_dot(xb, win_ref[:, d:2 * d]) + bin_ref[:, d:2 * d])
    mu = jnp.mean(v, axis=-1, keepdims=True)
    vc = v - mu
    var = jnp.mean(vc * vc, axis=-1, keepdims=True)
    vn = (vc * lax.rsqrt(var + EPS) * vg_ref[...] + vb_ref[...]).astype(bf16)
    r = lax.broadcasted_iota(jnp.int32, (CHUNK, CHUNK), 0)
    c = lax.broadcasted_iota(jnp.int32, (CHUNK, CHUNK), 1)
    causal = r >= c
    for hd in range(B_HEADS):
        w = jnp.where(causal, ws_ref[hd], 0.0).astype(bf16)
        cols = slice(hd * CHUNK, (hd + 1) * CHUNK)
        for ck in range(ts // CHUNK):
            rows = slice(ck * CHUNK, (ck + 1) * CHUNK)
            mixed_ref[rows, cols] = _dot(w, vn[rows, cols]) + bs_ref[:, cols]
    y = _dot((u * mixed_ref[...]).astype(bf16), wout_ref[...])
    x1 = x + y
    x1_ref[...] = x1
    lg_ref[...] = _router_logits(x1, ln2_ref[...], wr_ref[...])


def _mixer_b(x, ln, win, b_in, vg, vb, ws, bs_tile, wout, ln2, wr):
    t, d = x.shape
    ts = SEQ_TILE
    const = lambda *_: (0, 0)
    return pl.pallas_call(
        _mixer_b_kernel,
        out_shape=(jax.ShapeDtypeStruct((t, d), f32),
                   jax.ShapeDtypeStruct((LOGIT_ROWS, t), f32)),
        grid=(t // ts,),
        in_specs=[
            pl.BlockSpec((ts, d), lambda i: (i, 0)),
            pl.BlockSpec((1, d), const),
            pl.BlockSpec((d, 2 * d), const),
            pl.BlockSpec((1, 2 * d), const),
            pl.BlockSpec((1, d), const),
            pl.BlockSpec((1, d), const),
            pl.BlockSpec((B_HEADS, CHUNK, CHUNK), lambda i: (0, 0, 0)),
            pl.BlockSpec((CHUNK, d), const),
            pl.BlockSpec((d, d), const),
            pl.BlockSpec((1, d), const),
            pl.BlockSpec((LOGIT_ROWS, d), const),
        ],
        out_specs=(pl.BlockSpec((ts, d), lambda i: (i, 0)),
                   pl.BlockSpec((LOGIT_ROWS, ts), lambda i: (0, i))),
        scratch_shapes=[pltpu.VMEM((ts, d), f32)],
        compiler_params=pltpu.CompilerParams(
            dimension_semantics=("arbitrary",), vmem_limit_bytes=VMEM_LIMIT),
        name="mixer_b",
    )(x, ln, win, b_in, vg, vb, ws, bs_tile, wout, ln2, wr)


def _route_kernel(lg_ref, bias_ref, ri_ref, rg_ref, cnt_ref, run_ref):
    ts = lg_ref.shape[1]

    @pl.when(pl.program_id(0) == 0)
    def _():
        run_ref[...] = jnp.zeros_like(run_ref)

    lg = lg_ref[...] + bias_ref[...]
    gl = lg[0:N_GROUPS]
    gmax = jnp.max(gl, axis=0, keepdims=True)
    gi = lax.broadcasted_iota(jnp.int32, gl.shape, 0)
    g_idx = jnp.min(jnp.where(gl == gmax, gi, N_GROUPS), axis=0, keepdims=True)
    g_p = 1.0 / jnp.sum(jnp.exp(gl - gmax), axis=0, keepdims=True)

    sel = jnp.zeros((EXPERTS_PER_GROUP, ts), f32)
    for g in range(N_GROUPS):
        lo = EXPERT_ROW0 + g * EXPERTS_PER_GROUP
        sel = jnp.where(g_idx == g, lg[lo:lo + EXPERTS_PER_GROUP], sel)
    e = jnp.exp(sel - jnp.max(sel, axis=0, keepdims=True))
    p = e / jnp.sum(e, axis=0, keepdims=True)
    ei = lax.broadcasted_iota(jnp.int32, p.shape, 0)
    p1 = jnp.max(p, axis=0, keepdims=True)
    i1 = jnp.min(jnp.where(p == p1, ei, EXPERTS_PER_GROUP), axis=0, keepdims=True)
    pm = jnp.where(ei == i1, -1.0, p)
    p2 = jnp.max(pm, axis=0, keepdims=True)
    i2 = jnp.min(jnp.where(pm == p2, ei, EXPERTS_PER_GROUP), axis=0, keepdims=True)
    gate1 = g_p * p1 / (p1 + p2)
    gate2 = g_p * p2 / (p1 + p2)

    first_is_a = i1 < i2
    a = jnp.minimum(i1, i2)
    b = jnp.maximum(i1, i2)
    pair = ((a * (2 * EXPERTS_PER_GROUP - 1 - a)) >> 1) + (b - a - 1)
    bucket = g_idx * PAIRS_PER_GROUP + pair

    bi = lax.broadcasted_iota(jnp.int32, (BUCKET_ROWS, ts), 0)
    onehot = bi == bucket
    oh = onehot.astype(bf16)
    s_i = lax.broadcasted_iota(jnp.int32, (ts, ts), 0)
    t_i = lax.broadcasted_iota(jnp.int32, (ts, ts), 1)
    before = (s_i < t_i).astype(bf16)
    prefix = _dot(oh, before) + run_ref[...]
    rank = jnp.sum(jnp.where(onehot, prefix, 0.0), axis=0, keepdims=True)
    run_ref[...] += jnp.sum(oh.astype(f32), axis=1, keepdims=True)

    ri_ref[...] = jnp.zeros_like(ri_ref)
    ri_ref[0:1, :] = bucket
    ri_ref[1:2, :] = rank.astype(jnp.int32)
    rg_ref[...] = jnp.zeros_like(rg_ref)
    rg_ref[0:1, :] = jnp.where(first_is_a, gate1, gate2)
    rg_ref[1:2, :] = jnp.where(first_is_a, gate2, gate1)
    cnt_ref[...] = run_ref[...]


def _route(lg, bias):
    t = lg.shape[1]
    ts = SEQ_TILE
    return pl.pallas_call(
        _route_kernel,
        out_shape=(jax.ShapeDtypeStruct((8, t), jnp.int32),
                   jax.ShapeDtypeStruct((8, t), f32),
                   jax.ShapeDtypeStruct((BUCKET_ROWS, 1), f32)),
        grid=(t // ts,),
        in_specs=[pl.BlockSpec((LOGIT_ROWS, ts), lambda i: (0, i)),
                  pl.BlockSpec((LOGIT_ROWS, 1), lambda i: (0, 0))],
        out_specs=(pl.BlockSpec((8, ts), lambda i: (0, i)),
                   pl.BlockSpec((8, ts), lambda i: (0, i)),
                   pl.BlockSpec((BUCKET_ROWS, 1), lambda i: (0, 0))),
        scratch_shapes=[pltpu.VMEM((BUCKET_ROWS, 1), f32)],
        compiler_params=pltpu.CompilerParams(dimension_semantics=("arbitrary",)),
        name="route",
    )(lg, bias)


def _expert_kernel(ea_ref, eb_ref, nv_ref, tok_ref,
                   x_hbm, gs_ref, ln2_ref, wga_ref, wua_ref, wda_ref, wgb_ref, wub_ref, wdb_ref,
                   lnf_ref, out_hbm, xbuf, obuf, sem_in, sem_out, *, final):
    i = pl.program_id(0)
    bm = xbuf.shape[0]
    base = i * bm
    nv = nv_ref[i]

    def row_in(r):
        return pltpu.make_async_copy(x_hbm.at[pl.ds(tok_ref[base + r], 1)], xbuf.at[pl.ds(r, 1)], sem_in)

    def row_out(r):
        return pltpu.make_async_copy(obuf.at[pl.ds(r, 1)], out_hbm.at[pl.ds(tok_ref[base + r], 1)], sem_out)

    @pl.when(nv > 0)
    def _():
        lax.fori_loop(0, bm, lambda r, c: (row_in(r).start(), c)[1], 0)
        lax.fori_loop(0, bm, lambda r, c: (row_in(r).wait(), c)[1], 0)
        x = xbuf[...]
        xb = _rms(x, ln2_ref[...]).astype(bf16)
        gs = gs_ref[...]
        ha = jax.nn.silu(_dot(xb, wga_ref[...])) * _dot(xb, wua_ref[...]) * gs[:, 0:1]
        hb = jax.nn.silu(_dot(xb, wgb_ref[...])) * _dot(xb, wub_ref[...]) * gs[:, 1:2]
        y = _dot(ha.astype(bf16), wda_ref[...]) + _dot(hb.astype(bf16), wdb_ref[...])
        o = x + y
        if final:
            o = _rms(o, lnf_ref[...])
        obuf[...] = o
        lax.fori_loop(0, nv, lambda r, c: (row_out(r).start(), c)[1], 0)
        lax.fori_loop(0, nv, lambda r, c: (row_out(r).wait(), c)[1], 0)


def _experts(x, gate_slots, ln2, wg, wu, wd, lnf, ea, eb, nvalid, tok_of_slot, *, final):
    t, d = x.shape
    f = D_EXPERT
    bm = BLOCK_ROWS
    nb = ea.shape[0]
    wa = lambda i, ea, eb, nv, tok: (ea[i], 0, 0)
    wb = lambda i, ea, eb, nv, tok: (eb[i], 0, 0)
    const = lambda i, *_: (0, 0)
    grid_spec = pltpu.PrefetchScalarGridSpec(
        num_scalar_prefetch=4,
        grid=(nb,),
        in_specs=[
            pl.BlockSpec(memory_space=pl.ANY),
            pl.BlockSpec((bm, 2), lambda i, *_: (i, 0)),
            pl.BlockSpec((1, d), const),
            pl.BlockSpec((None, d, f), wa),
            pl.BlockSpec((None, d, f), wa),
            pl.BlockSpec((None, f, d), wa),
            pl.BlockSpec((None, d, f), wb),
            pl.BlockSpec((None, d, f), wb),
            pl.BlockSpec((None, f, d), wb),
            pl.BlockSpec((1, d), const),
        ],
        out_specs=pl.BlockSpec(memory_space=pl.ANY),
        scratch_shapes=[pltpu.VMEM((bm, d), f32), pltpu.VMEM((bm, d), f32),
                        pltpu.SemaphoreType.DMA, pltpu.SemaphoreType.DMA],
    )
    return pl.pallas_call(
        functools.partial(_expert_kernel, final=final),
        out_shape=jax.ShapeDtypeStruct((t, d), f32),
        grid_spec=grid_spec,
        compiler_params=pltpu.CompilerParams(
            dimension_semantics=("arbitrary",), vmem_limit_bytes=VMEM_LIMIT),
        name="experts_final" if final else "experts",
    )(ea, eb, nvalid, tok_of_slot, x, gate_slots, ln2, wg, wu, wd, wg, wu, wd, lnf)


def _plan_blocks(ri, rg, counts, t):
    bm = BLOCK_ROWS
    nb = t // bm + N_BUCKETS
    cnt = counts[:N_BUCKETS, 0].astype(jnp.int32)
    nblk = (cnt + bm - 1) // bm
    blk_end = jnp.cumsum(nblk)
    blk_start = blk_end - nblk
    slot = blk_start[ri[0]] * bm + ri[1]
    tok_of_slot = jnp.zeros((nb * bm,), jnp.int32).at[slot].set(jnp.arange(t, dtype=jnp.int32))
    blk = jnp.arange(nb, dtype=jnp.int32)
    used = blk < blk_end[-1]
    blk_c = jnp.where(used, blk, blk_end[-1] - 1)
    bkt = jnp.sum((blk_c[:, None] >= blk_end[None, :]).astype(jnp.int32), axis=1)
    bkt = jnp.minimum(bkt, N_BUCKETS - 1)
    nvalid = jnp.where(used, jnp.clip(cnt[bkt] - (blk - blk_start[bkt]) * bm, 0, bm), 0).astype(jnp.int32)
    grp = bkt // PAIRS_PER_GROUP
    pair = bkt % PAIRS_PER_GROUP
    ea = grp * EXPERTS_PER_GROUP + jnp.asarray(_PAIR_A)[pair]
    eb = grp * EXPERTS_PER_GROUP + jnp.asarray(_PAIR_B)[pair]
    gate_slots = jnp.stack([rg[0][tok_of_slot], rg[1][tok_of_slot]], axis=1)
    return ea, eb, nvalid, tok_of_slot, gate_slots


def _router_weights(w_group, b_group, w_expert, b_expert):
    d = w_group.shape[0]
    wr = jnp.zeros((LOGIT_ROWS, d), f32)
    wr = wr.at[0:N_GROUPS].set(w_group.T).at[EXPERT_ROW0:EXPERT_ROW0 + N_EXPERTS].set(w_expert.T)
    bias = jnp.zeros((LOGIT_ROWS, 1), f32)
    bias = bias.at[0:N_GROUPS, 0].set(b_group).at[EXPERT_ROW0:EXPERT_ROW0 + N_EXPERTS, 0].set(b_expert)
    return wr.astype(bf16), bias


def _moe(x, lg, bias, ln2, wg, wu, wd, lnf, *, final):
    t = x.shape[0]
    ri, rg, counts = _route(lg, bias)
    ea, eb, nvalid, tok_of_slot, gate_slots = _plan_blocks(ri, rg, counts, t)
    return _experts(x, gate_slots, ln2, wg.astype(bf16), wu.astype(bf16), wd.astype(bf16), lnf,
                    ea, eb, nvalid, tok_of_slot, final=final)


def kernel(x, ln_a, w_in_a, conv_w_a, w_out_a, ln_b, w_in_b, b_in_b, v_ln_g, v_ln_b, w_s, b_s, w_out_b, ln2, w_group, b_group, w_expert, b_expert, w_gate, w_up, w_down, ln_f):
    bsz, s, d = x.shape
    t = bsz * s
    row = lambda v: v.reshape(1, -1)
    lnf = row(ln_f)

    wr0, bias0 = _router_weights(w_group[0], b_group[0], w_expert[0], b_expert[0])
    wr1, bias1 = _router_weights(w_group[1], b_group[1], w_expert[1], b_expert[1])

    x1, lg0 = _mixer_a(x, row(ln_a[0]), w_in_a[0].astype(bf16), conv_w_a[0],
                       w_out_a[0].astype(bf16), row(ln2[0]), wr0)
    x2 = _moe(x1.reshape(t, d), lg0, bias0, row(ln2[0]), w_gate[0], w_up[0], w_down[0], lnf, final=False)

    bs_tile = jnp.repeat(b_s[0].T, CHUNK, axis=1)
    x3, lg1 = _mixer_b(x2, row(ln_b[0]), w_in_b[0].astype(bf16), row(b_in_b[0]), row(v_ln_g[0]),
                       row(v_ln_b[0]), w_s[0], bs_tile, w_out_b[0].astype(bf16), row(ln2[1]), wr1)
    out = _moe(x3, lg1, bias1, row(ln2[1]), w_gate[1], w_up[1], w_down[1], lnf, final=True)
    return out.reshape(bsz, s, d)
```

```python
import functools

import numpy as np
import jax
import jax.numpy as jnp
from jax import lax
from jax.experimental import pallas as pl
from jax.experimental.pallas import tpu as pltpu

D_MODEL = 1024
CONV_W = 3
CHUNK = 128
B_HEADS = 8
N_GROUPS = 4
EXPERTS_PER_GROUP = 8
N_EXPERTS = N_GROUPS * EXPERTS_PER_GROUP
D_EXPERT = 512
EPS = 1e-6

LANES = 128
PAIRS_PER_GROUP = EXPERTS_PER_GROUP * (EXPERTS_PER_GROUP - 1) // 2
N_BUCKETS = N_GROUPS * PAIRS_PER_GROUP
BUCKET_ROWS = 128
LOGIT_ROWS = 128
EXPERT_ROW0 = 8

SEQ_TILE = 512
LOG_BLOCK_ROWS = 7
BLOCK_ROWS = 1 << LOG_BLOCK_ROWS
ROW_W = D_MODEL + LANES
VMEM_LIMIT = 56 * 1024 * 1024

_PAIR_A = np.array([a for a in range(EXPERTS_PER_GROUP) for b in range(a + 1, EXPERTS_PER_GROUP)], np.int32)
_PAIR_B = np.array([b for a in range(EXPERTS_PER_GROUP) for b in range(a + 1, EXPERTS_PER_GROUP)], np.int32)

f32 = jnp.float32
bf16 = jnp.bfloat16
i32 = jnp.int32


def _num_blocks(t):
    return t // BLOCK_ROWS + N_BUCKETS


def _dot(a, b):
    return jnp.dot(a, b, preferred_element_type=f32)


def _rms(x, g):
    return x * lax.rsqrt(jnp.mean(x * x, axis=-1, keepdims=True) + EPS) * g


def _iota(shape, dim):
    return lax.broadcasted_iota(i32, shape, dim)


def _pick(onehot, col):
    return jnp.sum(jnp.where(onehot, col, 0), axis=0, keepdims=True)


def _route(lg):
    ts = lg.shape[1]
    gl = lg[0:N_GROUPS]
    gmax = jnp.max(gl, axis=0, keepdims=True)
    g_idx = jnp.min(jnp.where(gl == gmax, _iota(gl.shape, 0), N_GROUPS), axis=0, keepdims=True)
    g_p = 1.0 / jnp.sum(jnp.exp(gl - gmax), axis=0, keepdims=True)

    sel = jnp.zeros((EXPERTS_PER_GROUP, ts), f32)
    for g in range(N_GROUPS):
        lo = EXPERT_ROW0 + g * EXPERTS_PER_GROUP
        sel = jnp.where(g_idx == g, lg[lo:lo + EXPERTS_PER_GROUP], sel)
    e = jnp.exp(sel - jnp.max(sel, axis=0, keepdims=True))
    p = e / jnp.sum(e, axis=0, keepdims=True)
    ei = _iota(p.shape, 0)
    p1 = jnp.max(p, axis=0, keepdims=True)
    i1 = jnp.min(jnp.where(p == p1, ei, EXPERTS_PER_GROUP), axis=0, keepdims=True)
    pm = jnp.where(ei == i1, -1.0, p)
    p2 = jnp.max(pm, axis=0, keepdims=True)
    i2 = jnp.min(jnp.where(pm == p2, ei, EXPERTS_PER_GROUP), axis=0, keepdims=True)
    gate1 = g_p * p1 / (p1 + p2)
    gate2 = g_p * p2 / (p1 + p2)

    first_is_a = i1 < i2
    a = jnp.minimum(i1, i2)
    b = jnp.maximum(i1, i2)
    pair = ((a * (2 * EXPERTS_PER_GROUP - 1 - a)) >> 1) + (b - a - 1)
    bucket = g_idx * PAIRS_PER_GROUP + pair
    return bucket, jnp.where(first_is_a, gate1, gate2), jnp.where(first_is_a, gate2, gate1)


def _assign_slots(bucket, fill_ref, cur_ref, nfree_ref, bb_ref):
    ts = bucket.shape[1]
    bm = BLOCK_ROWS
    onehot = _iota((BUCKET_ROWS, ts), 0) == bucket
    oh = onehot.astype(bf16)
    before = (_iota((ts, ts), 0) < _iota((ts, ts), 1)).astype(bf16)
    rank = jnp.sum(jnp.where(onehot, _dot(oh, before), 0.0), axis=0, keepdims=True).astype(i32)
    count = jnp.sum(oh.astype(f32), axis=1, keepdims=True).astype(i32)

    fill, cur, nfree = fill_ref[...], cur_ref[...], nfree_ref[...]
    total = fill + count
    nnew = jnp.where(total > bm, (total - 1) >> LOG_BLOCK_ROWS, 0)
    lower = (_iota((BUCKET_ROWS, BUCKET_ROWS), 1) < _iota((BUCKET_ROWS, BUCKET_ROWS), 0)).astype(bf16)
    nnew_b = jnp.broadcast_to(nnew.astype(f32), (BUCKET_ROWS, LANES)).astype(bf16)
    newbase = nfree + _dot(lower, nnew_b)[:, 0:1].astype(i32)

    pos = _pick(onehot, fill) + rank
    over = pos - bm
    slot = jnp.where(
        pos < bm,
        _pick(onehot, cur) * bm + pos,
        (_pick(onehot, newbase) + (over >> LOG_BLOCK_ROWS)) * bm + (over & (bm - 1)))

    got_new = nnew > 0
    cur_ref[...] = jnp.where(got_new, newbase + nnew - 1, cur)
    fill_ref[...] = jnp.where(got_new, total - nnew * bm, total)
    nfree_ref[...] = nfree + jnp.sum(nnew, axis=0, keepdims=True)

    blk = _iota((BUCKET_ROWS, bb_ref.shape[1]), 1)
    is_new = (blk >= newbase) & (blk < newbase + nnew)
    bkt_col = _iota((BUCKET_ROWS, bb_ref.shape[1]), 0)
    new_bkt = jnp.sum(jnp.where(is_new, bkt_col, 0), axis=0, keepdims=True)
    any_new = jnp.sum(is_new.astype(i32), axis=0, keepdims=True) > 0
    bb_ref[...] = jnp.where(any_new, new_bkt, bb_ref[...])
    return slot


def _dispatch(x1, ln2_ref, wr_ref, bias_ref, tokl_init, xs_hbm, tokl_ref, bb_ref, nv_ref,
              rowbuf, slot_v, slot_s, row_sem, slot_sem, fill_ref, cur_ref, nfree_ref):
    i = pl.program_id(0)
    n = pl.num_programs(0)
    ts = x1.shape[0]
    par = i % 2

    @pl.when(i == 0)
    def _():
        init = pltpu.make_async_copy(tokl_init, tokl_ref, slot_sem)
        init.start()
        init.wait()
        fill_ref[...] = jnp.full(fill_ref.shape, BLOCK_ROWS, i32)
        cur_ref[...] = jnp.full(cur_ref.shape, -1, i32)
        nfree_ref[...] = jnp.zeros(nfree_ref.shape, i32)
        bb_ref[...] = jnp.zeros(bb_ref.shape, i32)

    xn = _rms(x1, ln2_ref[...]).astype(bf16)
    lg = lax.dot_general(wr_ref[...], xn, (((1,), (1,)), ((), ())), preferred_element_type=f32)
    bucket, gate_a, gate_b = _route(lg + bias_ref[...])
    slot = _assign_slots(bucket, fill_ref, cur_ref, nfree_ref, bb_ref)

    def all_rows(p):
        return pltpu.make_async_copy(rowbuf.at[p], xs_hbm.at[pl.ds(0, ts)], row_sem.at[p])

    @pl.when(i >= 2)
    def _():
        all_rows(par).wait()

    sub = _iota((LANES, ts), 0)
    gates = jnp.where(sub == 0, gate_a, jnp.where(sub == 1, gate_b, 0.0))
    rowbuf[par, :, 0:D_MODEL] = x1
    rowbuf[par, :, D_MODEL:ROW_W] = gates.T
    slot_v[...] = slot
    to_smem = pltpu.make_async_copy(slot_v, slot_s, slot_sem)
    to_smem.start()
    to_smem.wait()

    t0 = i * ts

    def issue(r, c):
        s = slot_s[0, r]
        pltpu.make_async_copy(rowbuf.at[par, pl.ds(r, 1)], xs_hbm.at[pl.ds(s, 1)], row_sem.at[par]).start()
        tokl_ref[s] = t0 + r
        return c

    lax.fori_loop(0, ts, issue, 0)

    @pl.when(i == n - 1)
    def _():
        blk = _iota((BUCKET_ROWS, nv_ref.shape[1]), 1)
        cur, fill = cur_ref[...], fill_ref[...]
        is_cur = (blk == cur) & (cur >= 0)
        open_rows = jnp.sum(jnp.where(is_cur, fill, 0), axis=0, keepdims=True)
        is_open = jnp.sum(is_cur.astype(i32), axis=0, keepdims=True) > 0
        allocated = blk[0:1] < nfree_ref[0:1]
        nv_ref[...] = jnp.where(allocated, jnp.where(is_open, open_rows, BLOCK_ROWS), 0)
        all_rows(1 - par).wait()
        all_rows(par).wait()


_DISPATCH_SCRATCH = lambda ts: [
    pltpu.VMEM((2, ts, ROW_W), f32),
    pltpu.VMEM((1, ts), i32),
    pltpu.SMEM((1, ts), i32),
    pltpu.SemaphoreType.DMA((2,)),
    pltpu.SemaphoreType.DMA,
    pltpu.VMEM((BUCKET_ROWS, 1), i32),
    pltpu.VMEM((BUCKET_ROWS, 1), i32),
    pltpu.VMEM((BUCKET_ROWS, 1), i32),
]


_DISPATCH_INIT_SPECS = [pl.BlockSpec(memory_space=pl.ANY), pl.BlockSpec(memory_space=pl.ANY)]


def _dispatch_out(t):
    nb = _num_blocks(t)
    nbp = -(-nb // LANES) * LANES
    shapes = (jax.ShapeDtypeStruct((nb * BLOCK_ROWS, ROW_W), f32),
              jax.ShapeDtypeStruct((nb * BLOCK_ROWS,), i32),
              jax.ShapeDtypeStruct((1, nbp), i32),
              jax.ShapeDtypeStruct((1, nbp), i32))
    specs = (pl.BlockSpec(memory_space=pl.ANY),
             pl.BlockSpec(memory_space=pltpu.SMEM),
             pl.BlockSpec((1, nbp), lambda i: (0, 0)),
             pl.BlockSpec((1, nbp), lambda i: (0, 0)))
    return shapes, specs


def _mixer_a_kernel(x_ref, ln_ref, win_ref, cw_ref, wout_ref, ln2_ref, wr_ref, bias_ref,
                    xs_init, tokl_init, xs_hbm, tokl_ref, bb_ref, nv_ref, zprev_ref, *scratch,
                    tiles_per_seq):
    del xs_init
    ts = x_ref.shape[0]
    d = D_MODEL

    @pl.when(pl.program_id(0) % tiles_per_seq == 0)
    def _():
        zprev_ref[...] = jnp.zeros_like(zprev_ref)

    x = x_ref[...]
    xb = _rms(x, ln_ref[...]).astype(bf16)
    b_gate = _dot(xb, win_ref[:, 0:d])
    c_gate = _dot(xb, win_ref[:, d:2 * d])
    h = _dot(xb, win_ref[:, 2 * d:3 * d])
    z = c_gate * h
    prev = zprev_ref[...]
    row = _iota((ts, 1), 0)
    z1 = jnp.where(row == 0, prev[7:8], pltpu.roll(z, 1, 0))
    z2 = jnp.where(row == 0, prev[6:7], jnp.where(row == 1, prev[7:8], pltpu.roll(z, 2, 0)))
    cw = cw_ref[...]
    conv = cw[0:1] * z2 + cw[1:2] * z1 + cw[2:3] * z
    zprev_ref[...] = z[ts - 8:ts]
    x1 = x + _dot((b_gate * conv).astype(bf16), wout_ref[...])
    _dispatch(x1, ln2_ref, wr_ref, bias_ref, tokl_init, xs_hbm, tokl_ref, bb_ref, nv_ref, *scratch)


def _mixer_a(x, ln, win, cw, wout, ln2, wr, bias, xs_init, tokl_init, tiles_per_seq):
    t, d = x.shape
    ts = SEQ_TILE
    const = lambda i: (0, 0)
    out_shape, out_specs = _dispatch_out(t)
    return pl.pallas_call(
        functools.partial(_mixer_a_kernel, tiles_per_seq=tiles_per_seq),
        out_shape=out_shape,
        grid=(t // ts,),
        in_specs=[
            pl.BlockSpec((ts, d), lambda i: (i, 0)),
            pl.BlockSpec((1, d), const),
            pl.BlockSpec((d, 3 * d), const),
            pl.BlockSpec((CONV_W, d), const),
            pl.BlockSpec((d, d), const),
            pl.BlockSpec((1, d), const),
            pl.BlockSpec((LOGIT_ROWS, d), const),
            pl.BlockSpec((LOGIT_ROWS, 1), const),
        ] + _DISPATCH_INIT_SPECS,
        out_specs=out_specs,
        input_output_aliases={8: 0},
        scratch_shapes=[pltpu.VMEM((8, d), f32)] + _DISPATCH_SCRATCH(ts),
        compiler_params=pltpu.CompilerParams(
            dimension_semantics=("arbitrary",), vmem_limit_bytes=VMEM_LIMIT),
        name="mixer_a",
    )(x, ln, win, cw, wout, ln2, wr, bias, xs_init, tokl_init)


def _mixer_b_kernel(x_ref, ln_ref, win_ref, bin_ref, vg_ref, vb_ref, ws_ref, bs_ref,
                    wout_ref, ln2_ref, wr_ref, bias_ref,
                    xs_init, tokl_init, xs_hbm, tokl_ref, bb_ref, nv_ref, mixed_ref, *scratch):
    del xs_init
    ts = x_ref.shape[0]
    d = D_MODEL
    x = x_ref[...]
    xb = _rms(x, ln_ref[...]).astype(bf16)
    u = jax.nn.gelu(_dot(xb, win_ref[:, 0:d]) + bin_ref[:, 0:d])
    v = jax.nn.gelu(_dot(xb, win_ref[:, d:2 * d]) + bin_ref[:, d:2 * d])
    mu = jnp.mean(v, axis=-1, keepdims=True)
    vc = v - mu
    var = jnp.mean(vc * vc, axis=-1, keepdims=True)
    vn = (vc * lax.rsqrt(var + EPS) * vg_ref[...] + vb_ref[...]).astype(bf16)
    causal = _iota((CHUNK, CHUNK), 0) >= _iota((CHUNK, CHUNK), 1)
    for hd in range(B_HEADS):
        w = jnp.where(causal, ws_ref[hd], 0.0).astype(bf16)
        cols = slice(hd * CHUNK, (hd + 1) * CHUNK)
        for ck in range(ts // CHUNK):
            rows = slice(ck * CHUNK, (ck + 1) * CHUNK)
            mixed_ref[rows, cols] = _dot(w, vn[rows, cols]) + bs_ref[:, cols]
    x1 = x + _dot((u * mixed_ref[...]).astype(bf16), wout_ref[...])
    _dispatch(x1, ln2_ref, wr_ref, bias_ref, tokl_init, xs_hbm, tokl_ref, bb_ref, nv_ref, *scratch)


def _mixer_b(x, ln, win, b_in, vg, vb, ws, bs_tile, wout, ln2, wr, bias, xs_init, tokl_init):
    t, d = x.shape
    ts = SEQ_TILE
    const = lambda i: (0, 0)
    out_shape, out_specs = _dispatch_out(t)
    return pl.pallas_call(
        _mixer_b_kernel,
        out_shape=out_shape,
        grid=(t // ts,),
        in_specs=[
            pl.BlockSpec((ts, d), lambda i: (i, 0)),
            pl.BlockSpec((1, d), const),
            pl.BlockSpec((d, 2 * d), const),
            pl.BlockSpec((1, 2 * d), const),
            pl.BlockSpec((1, d), const),
            pl.BlockSpec((1, d), const),
            pl.BlockSpec((B_HEADS, CHUNK, CHUNK), lambda i: (0, 0, 0)),
            pl.BlockSpec((CHUNK, d), const),
            pl.BlockSpec((d, d), const),
            pl.BlockSpec((1, d), const),
            pl.BlockSpec((LOGIT_ROWS, d), const),
            pl.BlockSpec((LOGIT_ROWS, 1), const),
        ] + _DISPATCH_INIT_SPECS,
        out_specs=out_specs,
        input_output_aliases={12: 0},
        scratch_shapes=[pltpu.VMEM((ts, d), f32)] + _DISPATCH_SCRATCH(ts),
        compiler_params=pltpu.CompilerParams(
            dimension_semantics=("arbitrary",), vmem_limit_bytes=VMEM_LIMIT),
        name="mixer_b",
    )(x, ln, win, b_in, vg, vb, ws, bs_tile, wout, ln2, wr, bias, xs_init, tokl_init)


def _expert_kernel(order_ref, ea_ref, eb_ref, nv_ref, tokl_ref,
                   xs_ref, ln2_ref, wga_ref, wua_ref, wda_ref, wgb_ref, wub_ref, wdb_ref,
                   lnf_ref, out_hbm, obuf, sem, pend_ref, *, final):
    i = pl.program_id(0)
    n = pl.num_programs(0)
    bm = BLOCK_ROWS
    par = i % 2
    nv = nv_ref[i]
    base = order_ref[i] * bm

    def row_out(p, r, tok):
        return pltpu.make_async_copy(obuf.at[p, pl.ds(r, 1)], out_hbm.at[pl.ds(tok, 1)], sem.at[p])

    def drain(p):
        lax.fori_loop(0, pend_ref[p], lambda r, c: (row_out(p, 0, 0).wait(), c)[1], 0)
        pend_ref[p] = 0

    @pl.when(i == 0)
    def _():
        pend_ref[0] = 0
        pend_ref[1] = 0

    @pl.when(nv > 0)
    def _():
        drain(par)
        xw = xs_ref[...]
        x = xw[:, 0:D_MODEL]
        xb = _rms(x, ln2_ref[...]).astype(bf16)
        ha = jax.nn.silu(_dot(xb, wga_ref[...])) * _dot(xb, wua_ref[...]) * xw[:, D_MODEL:D_MODEL + 1]
        hb = jax.nn.silu(_dot(xb, wgb_ref[...])) * _dot(xb, wub_ref[...]) * xw[:, D_MODEL + 1:D_MODEL + 2]
        o = x + _dot(ha.astype(bf16), wda_ref[...]) + _dot(hb.astype(bf16), wdb_ref[...])
        if final:
            o = _rms(o, lnf_ref[...])
        obuf[par] = o

        def issue(r, c):
            row_out(par, r, tokl_ref[base + r]).start()
            return c

        lax.fori_loop(0, nv, issue, 0)
        pend_ref[par] = nv

    @pl.when(i == n - 1)
    def _():
        drain(0)
        drain(1)


def _experts(xs, tokl, order, ea, eb, nvalid, ln2, wg, wu, wd, lnf, t, *, final):
    d = D_MODEL
    f = D_EXPERT
    bm = BLOCK_ROWS
    nb = order.shape[0]
    wa = lambda i, order, ea, eb, nv, tokl: (ea[i], 0, 0)
    wb = lambda i, order, ea, eb, nv, tokl: (eb[i], 0, 0)
    const = lambda i, *_: (0, 0)
    grid_spec = pltpu.PrefetchScalarGridSpec(
        num_scalar_prefetch=5,
        grid=(nb,),
        in_specs=[
            pl.BlockSpec((bm, ROW_W), lambda i, order, *_: (order[i], 0)),
            pl.BlockSpec((1, d), const),
            pl.BlockSpec((None, d, f), wa),
            pl.BlockSpec((None, d, f), wa),
            pl.BlockSpec((None, f, d), wa),
            pl.BlockSpec((None, d, f), wb),
            pl.BlockSpec((None, d, f), wb),
            pl.BlockSpec((None, f, d), wb),
            pl.BlockSpec((1, d), const),
        ],
        out_specs=pl.BlockSpec(memory_space=pl.ANY),
        scratch_shapes=[pltpu.VMEM((2, bm, d), f32), pltpu.SemaphoreType.DMA((2,)),
                        pltpu.SMEM((2,), i32)],
    )
    return pl.pallas_call(
        functools.partial(_expert_kernel, final=final),
        out_shape=jax.ShapeDtypeStruct((t, d), f32),
        grid_spec=grid_spec,
        compiler_params=pltpu.CompilerParams(
            dimension_semantics=("arbitrary",), vmem_limit_bytes=VMEM_LIMIT),
        name="experts_final" if final else "experts",
    )(order, ea, eb, nvalid, tokl, xs, ln2, wg, wu, wd, wg, wu, wd, lnf)


def _plan_blocks(bb, nvalid, nb):
    bb = bb[0, :nb]
    nvl = nvalid[0, :nb]
    j = jnp.arange(nb, dtype=i32)
    used = nvl > 0
    n_used = jnp.sum(used.astype(i32))
    key = jnp.where(used, bb, BUCKET_ROWS)
    ahead = (key[None, :] < key[:, None]) | ((key[None, :] == key[:, None]) & (j[None, :] < j[:, None]))
    pos = jnp.sum(ahead.astype(i32), axis=1)
    step = jnp.minimum(j, n_used - 1)
    hit = (pos[None, :] == step[:, None])
    pick = lambda v: jnp.sum(jnp.where(hit, v[None, :], 0), axis=1)
    order = pick(j)
    bkt = pick(bb)
    nv_step = jnp.where(j < n_used, pick(nvl), 0)
    grp = bkt // PAIRS_PER_GROUP
    pair = bkt % PAIRS_PER_GROUP
    pair_hit = pair[:, None] == jnp.arange(PAIRS_PER_GROUP, dtype=i32)[None, :]
    ea = grp * EXPERTS_PER_GROUP + jnp.sum(jnp.where(pair_hit, jnp.asarray(_PAIR_A)[None, :], 0), axis=1)
    eb = grp * EXPERTS_PER_GROUP + jnp.sum(jnp.where(pair_hit, jnp.asarray(_PAIR_B)[None, :], 0), axis=1)
    return order.astype(i32), ea.astype(i32), eb.astype(i32), nv_step.astype(i32)


def _router_weights(w_group, b_group, w_expert, b_expert):
    d = w_group.shape[0]
    wr = jnp.zeros((LOGIT_ROWS, d), f32)
    wr = wr.at[0:N_GROUPS].set(w_group.T).at[EXPERT_ROW0:EXPERT_ROW0 + N_EXPERTS].set(w_expert.T)
    bias = jnp.zeros((LOGIT_ROWS, 1), f32)
    bias = bias.at[0:N_GROUPS, 0].set(b_group).at[EXPERT_ROW0:EXPERT_ROW0 + N_EXPERTS, 0].set(b_expert)
    return wr.astype(bf16), bias


def _moe(dispatched, ln2, wg, wu, wd, lnf, t, *, final):
    xs, tokl, bb, nvalid = dispatched
    order, ea, eb, nv_step = _plan_blocks(bb, nvalid, _num_blocks(t))
    return _experts(xs, tokl, order, ea, eb, nv_step, ln2,
                    wg.astype(bf16), wu.astype(bf16), wd.astype(bf16), lnf, t, final=final)


def kernel(x, ln_a, w_in_a, conv_w_a, w_out_a, ln_b, w_in_b, b_in_b, v_ln_g, v_ln_b, w_s, b_s, w_out_b, ln2, w_group, b_group, w_expert, b_expert, w_gate, w_up, w_down, ln_f):
    bsz, s, d = x.shape
    t = bsz * s
    row = lambda v: v.reshape(1, -1)
    lnf = row(ln_f)

    wr0, bias0 = _router_weights(w_group[0], b_group[0], w_expert[0], b_expert[0])
    wr1, bias1 = _router_weights(w_group[1], b_group[1], w_expert[1], b_expert[1])

    n_slots = _num_blocks(t) * BLOCK_ROWS
    disp0 = _mixer_a(x.reshape(t, d), row(ln_a[0]), w_in_a[0].astype(bf16), conv_w_a[0],
                     w_out_a[0].astype(bf16), row(ln2[0]), wr0, bias0,
                     jnp.zeros((n_slots, ROW_W), f32), jnp.zeros((n_slots,), i32), s // SEQ_TILE)
    x2 = _moe(disp0, row(ln2[0]), w_gate[0], w_up[0], w_down[0], lnf, t, final=False)

    bs_tile = jnp.repeat(b_s[0].T, CHUNK, axis=1)
    disp1 = _mixer_b(x2, row(ln_b[0]), w_in_b[0].astype(bf16), row(b_in_b[0]), row(v_ln_g[0]),
                     row(v_ln_b[0]), w_s[0], bs_tile, w_out_b[0].astype(bf16), row(ln2[1]), wr1, bias1,
                     disp0[0], disp0[1])
    out = _moe(disp1, row(ln2[1]), w_gate[1], w_up[1], w_down[1], lnf, t, final=True)
    return out.reshape(bsz, s, d)
```
